```python
import math
import jax, jax.numpy as jnp
from jax import lax
import numpy as np

D_MODEL = 1024
BATCH = 2
SEQ = 16384
DEPTH = 2
DEC_BATCH = 32
DEC_SEQ = 32
PAST_LEN = 4096

CHUNK = 64
M_HEADS = 4
M_HD = 128
M_WIDTH = M_HEADS * M_HD
A_HEADS = 8
A_KV = 2
A_HD = 64
A_GROUP = A_HEADS // A_KV
A_WIDTH = A_HEADS * A_HD
WINDOW = 128
WIN_CHUNKS = WINDOW // CHUNK
MIX_WIDTH = M_WIDTH + A_WIDTH
N_BUCKETS = 32
MAX_DISTANCE = 128
PEER_HEADS = 8
N_KEYS = 128
N_EXPERTS = N_KEYS * N_KEYS
PEER_TOPK = 16
PEER_DK = 256
PEER_DK_HALF = PEER_DK // 2
PEER_BLOCK = 256
EPS = 1e-6
IN_SIZES = (M_WIDTH, M_WIDTH, M_WIDTH, M_WIDTH, M_HEADS, M_HEADS, A_WIDTH, A_KV * A_HD, A_KV * A_HD)
IN_COLS = sum(IN_SIZES)
IN_SPLITS = tuple(int(s) for s in np.cumsum(IN_SIZES)[:-1])

kernel_name = 'hybrid_mlstm_swa_peer_stream_step'


def rms_norm(x, g):
    xf = x.astype(jnp.float32)
    y = xf * lax.rsqrt(jnp.mean(xf * xf, axis=-1, keepdims=True) + EPS)
    return (y * g.astype(jnp.float32)).astype(x.dtype)


def t5_bucket(rel):
    n = -rel
    half = N_BUCKETS // 2
    ret = jnp.where(n < 0, half, 0)
    n = jnp.abs(n)
    max_exact = half // 2
    large = max_exact + (jnp.log(jnp.maximum(n, 1).astype(jnp.float32) / max_exact)
                         / math.log(MAX_DISTANCE / max_exact) * (half - max_exact)).astype(jnp.int32)
    large = jnp.minimum(large, half - 1)
    return ret + jnp.where(n < max_exact, n, large)


def swa_bias(rel_bias, q_len, prev_len):
    rel = jnp.arange(prev_len + q_len)[None, :] - prev_len - jnp.arange(q_len)[:, None]
    b = rel_bias[t5_bucket(rel)].astype(jnp.float32)
    return jnp.transpose(b, (2, 0, 1)).reshape(A_KV, A_GROUP, q_len, prev_len + q_len)


def mlstm_chunk(carry, xs):
    C0, n0, m0 = carry
    q, k, v, ig, lf = xs
    L = q.shape[2]
    b = jnp.cumsum(lf, axis=-1)
    causal = jnp.tril(jnp.ones((L, L), bool))
    log_d = jnp.where(causal, b[..., :, None] - b[..., None, :] + ig[..., None, :], -jnp.inf)
    m_inter = b + m0[..., None]
    m_t = jnp.maximum(m_inter, jnp.max(log_d, axis=-1))
    s = jnp.einsum('bhld,bhsd->bhls', q, k) * jnp.exp(log_d - m_t[..., None])
    a = jnp.exp(m_inter - m_t)
    num = a[..., None] * jnp.einsum('bhvk,bhlk->bhlv', C0, q) + jnp.einsum('bhls,bhsv->bhlv', s, v)
    den = a * jnp.einsum('bhk,bhlk->bhl', n0, q) + jnp.sum(s, axis=-1)
    h = num / jnp.maximum(jnp.abs(den), jnp.exp(-m_t))[..., None]
    g = b[..., -1:] - b + ig
    m_new = jnp.maximum(b[..., -1] + m0, jnp.max(g, axis=-1))
    w = jnp.exp(g - m_new[..., None])
    a_end = jnp.exp(b[..., -1] + m0 - m_new)
    C_new = a_end[..., None, None] * C0 + jnp.einsum('bhs,bhsv,bhsk->bhvk', w, v, k)
    n_new = a_end[..., None] * n0 + jnp.einsum('bhs,bhsk->bhk', w, k)
    return (C_new, n_new, m_new), h


def token_mixers(h, w_in_l, b_i, b_f, mnorm_l, sinks_l, rel_bias, w_out_l, C0, n0, m0, ck, cv, prompt):
    f32 = jnp.float32
    B, S, _ = h.shape
    L = CHUNK if prompt else S
    NC = S // L
    q_m, k_m, v_m, o_m, i_m, f_m, q_a, k_a, v_a = jnp.split(h @ w_in_l, list(IN_SPLITS), axis=-1)

    def heads(t):
        return t.reshape(B, S, M_HEADS, M_HD).transpose(0, 2, 1, 3).astype(f32)

    def chunks(t):
        t = t.reshape(t.shape[:2] + (NC, L) + t.shape[3:])
        return jnp.moveaxis(t, 2, 0)

    qm = heads(q_m)
    km = heads(k_m) * (M_HD ** -0.5)
    vm = heads(v_m)
    ig = (i_m + b_i).astype(f32).transpose(0, 2, 1)
    lf = jax.nn.log_sigmoid((f_m + b_f).astype(f32)).transpose(0, 2, 1)
    xs = (chunks(qm), chunks(km), chunks(vm), chunks(ig), chunks(lf))
    (C1, n1, m1), hm = lax.scan(mlstm_chunk, (C0, n0, m0), xs)
    hm = jnp.moveaxis(hm, 0, 2).reshape(B, M_HEADS, S, M_HD).transpose(0, 2, 1, 3)
    hm = hm * lax.rsqrt(jnp.mean(hm * hm, axis=-1, keepdims=True) + EPS) * mnorm_l.reshape(M_HEADS, M_HD).astype(f32)
    hm = (hm * jax.nn.sigmoid(o_m.reshape(B, S, M_HEADS, M_HD).astype(f32))).reshape(B, S, M_WIDTH).astype(h.dtype)

    qa = q_a.reshape(B, NC, L, A_KV, A_GROUP, A_HD)
    ka = k_a.reshape(B, S, A_KV, A_HD)
    va = v_a.reshape(B, S, A_KV, A_HD)
    if prompt:
        P = WINDOW

        def prev(t):
            tp = jnp.pad(t.reshape(B, NC, L, A_KV, A_HD), ((0, 0), (WIN_CHUNKS, 0), (0, 0), (0, 0), (0, 0)))
            return jnp.concatenate([tp[:, w:w + NC] for w in range(WIN_CHUNKS)], axis=2)

        kprev, vprev = prev(ka), prev(va)
        valid = (jnp.arange(NC)[:, None] - WIN_CHUNKS) * L + jnp.arange(P)[None, :] >= 0
        new_k, new_v = ka[:, -WINDOW:], va[:, -WINDOW:]
    else:
        P = ck.shape[1]
        ckx, cvx = ck.astype(ka.dtype), cv.astype(va.dtype)
        kprev, vprev = ckx[:, None], cvx[:, None]
        valid = jnp.ones((1, P), bool)
        new_k = jnp.concatenate([ckx, ka], axis=1)[:, -P:]
        new_v = jnp.concatenate([cvx, va], axis=1)[:, -P:]
    keys = jnp.concatenate([kprev, ka.reshape(B, NC, L, A_KV, A_HD)], axis=2)
    vals = jnp.concatenate([vprev, va.reshape(B, NC, L, A_KV, A_HD)], axis=2)
    scores = jnp.einsum('bnqkgd,bnskd->bnkgqs', qa, keys, preferred_element_type=f32) * (A_HD ** -0.5)
    scores = scores + swa_bias(rel_bias, L, P)[None, None]
    mask = jnp.concatenate([valid, jnp.ones((valid.shape[0], L), bool)], axis=1)
    scores = jnp.where(mask[None, :, None, None, None, :], scores, -1e30)
    sink = sinks_l.astype(f32).reshape(A_KV, A_GROUP)[None, None, :, :, None]
    mx = jnp.maximum(jnp.max(scores, axis=-1), sink)
    p = jnp.exp(scores - mx[..., None])
    denom = jnp.sum(p, axis=-1) + jnp.exp(sink - mx)
    ha = jnp.einsum('bnkgqs,bnskd->bnkgqd', p, vals.astype(f32)) / denom[..., None]
    ha = jnp.transpose(ha, (0, 1, 4, 2, 3, 5)).reshape(B, S, A_WIDTH).astype(h.dtype)

    y = jnp.concatenate([hm, ha], axis=-1) @ w_out_l
    return y, C1, n1, m1, new_k, new_v


def peer_ffn(h, wq, k1, k2, u, v):
    B, S, D = h.shape
    n = B * S
    nb = -(-n // PEER_BLOCK)
    flat = jnp.pad(h.reshape(n, D), ((0, nb * PEER_BLOCK - n), (0, 0))).reshape(nb, PEER_BLOCK, D)

    def block(xb):
        q = (xb @ wq).reshape(PEER_BLOCK, PEER_HEADS, 2, PEER_DK_HALF)
        s1 = jnp.einsum('thd,hkd->thk', q[:, :, 0], k1)
        s2 = jnp.einsum('thd,hkd->thk', q[:, :, 1], k2)
        v1, i1 = lax.top_k(s1, PEER_TOPK)
        v2, i2 = lax.top_k(s2, PEER_TOPK)
        cand = (v1[..., :, None] + v2[..., None, :]).reshape(PEER_BLOCK, PEER_HEADS, PEER_TOPK * PEER_TOPK)
        cidx = (i1[..., :, None] * N_KEYS + i2[..., None, :]).reshape(PEER_BLOCK, PEER_HEADS, PEER_TOPK * PEER_TOPK)
        sc, pos = lax.top_k(cand, PEER_TOPK)
        eidx = jnp.take_along_axis(cidx, pos, axis=-1)
        g = jax.nn.softmax(sc.astype(jnp.float32), axis=-1).astype(xb.dtype)
        act = jax.nn.gelu(jnp.einsum('thed,td->the', u[eidx], xb))
        return jnp.einsum('the,thed->td', g * act, v[eidx])

    out = lax.map(block, flat).reshape(nb * PEER_BLOCK, D)[:n]
    return out.reshape(B, S, D)


def trunk(x, c, cache_k, cache_v, st_C, st_n, st_m, w_ada, b_ada, norm1_g, norm2_g, w_in, b_igate, b_fgate,
          mnorm_g, sinks, rel_bias, w_out, peer_wq, peer_k1, peer_k2, peer_u, peer_v, final_g, prompt):
    B = x.shape[0]
    f32 = jnp.float32
    cs = jax.nn.silu(c)
    outs_C, outs_n, outs_m, outs_k, outs_v = [], [], [], [], []
    for l in range(DEPTH):
        mod = (cs @ w_ada[l] + b_ada[l])[:, None, :]
        shift1, scale1, gate1, shift2, scale2, gate2 = jnp.split(mod, 6, axis=-1)
        h = rms_norm(x, norm1_g[l]) * (1 + scale1) + shift1
        if prompt:
            C0 = jnp.zeros((B, M_HEADS, M_HD, M_HD), f32)
            n0 = jnp.zeros((B, M_HEADS, M_HD), f32)
            m0 = jnp.zeros((B, M_HEADS), f32)
            ck, cv = None, None
        else:
            C0, n0, m0 = st_C[l].astype(f32), st_n[l].astype(f32), st_m[l].astype(f32)
            ck, cv = cache_k[l], cache_v[l]
        y, C1, n1, m1, k1, v1 = token_mixers(h, w_in[l], b_igate[l], b_fgate[l], mnorm_g[l], sinks[l], rel_bias,
                                             w_out[l], C0, n0, m0, ck, cv, prompt)
        x = x + gate1 * y
        h = rms_norm(x, norm2_g[l]) * (1 + scale2) + shift2
        x = x + gate2 * peer_ffn(h, peer_wq[l], peer_k1[l], peer_k2[l], peer_u[l], peer_v[l])
        outs_C.append(C1)
        outs_n.append(n1)
        outs_m.append(m1)
        outs_k.append(k1)
        outs_v.append(v1)
    return (rms_norm(x, final_g), jnp.stack(outs_C), jnp.stack(outs_n), jnp.stack(outs_m),
            jnp.stack(outs_k), jnp.stack(outs_v))


def setup_inputs(seed: int = 0) -> dict:
    key = jax.random.key(seed)
    ks = jax.random.split(key, 32)
    f32 = jnp.float32

    def nrm(k, shape, s):
        return jax.random.normal(k, shape, f32) * s

    rows = min(WINDOW, PAST_LEN)
    D = D_MODEL
    return {
        'x_prompt': nrm(ks[0], (BATCH, SEQ, D), 1.0),
        'x_sample': nrm(ks[1], (DEC_BATCH, DEC_SEQ, D), 1.0),
        'cache_k': nrm(ks[2], (DEPTH, DEC_BATCH, rows, A_KV, A_HD), 1.0),
        'cache_v': nrm(ks[3], (DEPTH, DEC_BATCH, rows, A_KV, A_HD), 1.0),
        'state_C': nrm(ks[4], (DEPTH, DEC_BATCH, M_HEADS, M_HD, M_HD), 0.1),
        'state_n': nrm(ks[5], (DEPTH, DEC_BATCH, M_HEADS, M_HD), 0.1),
        'state_m': nrm(ks[6], (DEPTH, DEC_BATCH, M_HEADS), 1.0),
        'c_prompt': nrm(ks[7], (BATCH, D), 1.0),
        'c_sample': nrm(ks[8], (DEC_BATCH, D), 1.0),
        'w_ada': nrm(ks[9], (DEPTH, D, 6 * D), 0.02),
        'b_ada': nrm(ks[10], (DEPTH, 6 * D), 0.02),
        'norm1_g': 1.0 + nrm(ks[11], (DEPTH, D), 0.05),
        'norm2_g': 1.0 + nrm(ks[12], (DEPTH, D), 0.05),
        'w_in': nrm(ks[13], (DEPTH, D, IN_COLS), D ** -0.5),
        'b_igate': nrm(ks[14], (DEPTH, M_HEADS), 0.1),
        'b_fgate': jnp.linspace(3.0, 6.0, M_HEADS, dtype=f32)[None, :] + nrm(ks[15], (DEPTH, M_HEADS), 0.1),
        'mnorm_g': 1.0 + nrm(ks[16], (DEPTH, M_WIDTH), 0.05),
        'sinks': nrm(ks[17], (DEPTH, A_HEADS), 0.5),
        'rel_bias': nrm(ks[18], (N_BUCKETS, A_HEADS), 0.5),
        'w_out': nrm(ks[19], (DEPTH, MIX_WIDTH, D), MIX_WIDTH ** -0.5),
        'peer_wq': nrm(ks[20], (DEPTH, D, PEER_HEADS * PEER_DK), D ** -0.5),
        'peer_k1': nrm(ks[21], (DEPTH, PEER_HEADS, N_KEYS, PEER_DK_HALF), PEER_DK_HALF ** -0.5),
        'peer_k2': nrm(ks[22], (DEPTH, PEER_HEADS, N_KEYS, PEER_DK_HALF), PEER_DK_HALF ** -0.5),
        'peer_u': nrm(ks[23], (DEPTH, N_EXPERTS, D), D ** -0.5),
        'peer_v': nrm(ks[24], (DEPTH, N_EXPERTS, D), PEER_HEADS ** -0.5),
        'final_g': 1.0 + nrm(ks[25], (D,), 0.05),
    }


def reference(x_prompt, x_sample, cache_k, cache_v, state_C, state_n, state_m, c_prompt, c_sample,
              w_ada, b_ada, norm1_g, norm2_g, w_in, b_igate, b_fgate, mnorm_g, sinks, rel_bias, w_out,
              peer_wq, peer_k1, peer_k2, peer_u, peer_v, final_g):
    y_prompt, p_C, p_n, p_m, p_k, p_v = trunk(
        x_prompt, c_prompt, None, None, None, None, None,
        w_ada, b_ada, norm1_g, norm2_g, w_in, b_igate, b_fgate, mnorm_g, sinks, rel_bias, w_out,
        peer_wq, peer_k1, peer_k2, peer_u, peer_v, final_g, prompt=True)
    y_sample, s_C, s_n, s_m, s_k, s_v = trunk(
        x_sample, c_sample, cache_k, cache_v, state_C, state_n, state_m,
        w_ada, b_ada, norm1_g, norm2_g, w_in, b_igate, b_fgate, mnorm_g, sinks, rel_bias, w_out,
        peer_wq, peer_k1, peer_k2, peer_u, peer_v, final_g, prompt=False)
    return (y_prompt, y_sample, p_C, p_n, p_m, p_k, p_v, s_C, s_n, s_m, s_k, s_v)
```

```python
import functools
import math

import jax
import jax.numpy as jnp
from jax import lax
from jax.experimental import pallas as pl
from jax.experimental.pallas import tpu as pltpu

F32, BF16 = jnp.float32, jnp.bfloat16
EPS = 1e-6
NEG = -1e30

M_HEADS, M_HD = 4, 128
M_WIDTH = M_HEADS * M_HD
A_HEADS, A_KV, A_HD = 8, 2, 64
A_GROUP = A_HEADS // A_KV
A_WIDTH = A_HEADS * A_HD
A_KVW = A_KV * A_HD
CHUNK, WINDOW = 64, 128
N_BUCKETS, MAX_DISTANCE = 32, 128
PEER_HEADS, N_KEYS, PEER_TOPK, PEER_DKH = 8, 128, 16, 128
GATE_PAD = 128

TOK_TILE = 256
PEER_TOK = 512
PEER_EB = 1024
PEER_ROWS = 16
MLSTM_CHUNK = 256
SWA_QB = 256
VMEM_LIMIT = 56 * 1024 * 1024


def _params(*sem):
    return pltpu.CompilerParams(dimension_semantics=sem, vmem_limit_bytes=VMEM_LIMIT)


def _dot(a, b):
    return jnp.dot(a, b, preferred_element_type=F32)


def _dot_nt(a, b):
    return lax.dot_general(a, b, (((1,), (1,)), ((), ())), preferred_element_type=F32)


def _dot_tn(a, b):
    return lax.dot_general(a, b, (((0,), (0,)), ((), ())), preferred_element_type=F32)


def _split2(x):
    hi = x.astype(BF16)
    lo = (x - hi.astype(F32)).astype(BF16)
    return hi, lo


def _dot3(a, w):
    ah, al = _split2(a)
    wh, wl = _split2(w)
    return _dot(ah, wh) + (_dot(ah, wl) + _dot(al, wh))


def _rms(x):
    return x * lax.rsqrt(jnp.mean(x * x, axis=-1, keepdims=True) + EPS)


def _log_sigmoid(x):
    return jnp.minimum(x, 0.0) - jnp.log1p(jnp.exp(-jnp.abs(x)))


def _ada_kernel(c_ref, w_ref, b_ref, o_ref):
    c = c_ref[...]
    cs = c * jax.nn.sigmoid(c)
    o_ref[0] = _dot3(cs, w_ref[0]) + b_ref[0]


def _ada(c_all, w_ada, b_ada):
    depth, d, n6 = w_ada.shape
    m = c_all.shape[0]
    bn = 1024
    return pl.pallas_call(
        _ada_kernel,
        grid=(depth, n6 // bn),
        in_specs=[pl.BlockSpec((m, d), lambda l, j: (0, 0)),
                  pl.BlockSpec((1, d, bn), lambda l, j: (l, 0, j)),
                  pl.BlockSpec((1, 1, bn), lambda l, j: (l, 0, j))],
        out_specs=pl.BlockSpec((1, m, bn), lambda l, j: (l, 0, j)),
        out_shape=jax.ShapeDtypeStruct((depth, m, n6), F32),
        compiler_params=_params("arbitrary", "arbitrary"),
        name="adaln",
    )(c_all, w_ada, b_ada.reshape(depth, 1, n6))


def _bias_kernel(rb_ref, oh_ref, o_ref):
    x = rb_ref[...]
    hi = x.astype(BF16)
    r1 = x - hi.astype(F32)
    mid = r1.astype(BF16)
    lo = (r1 - mid.astype(F32)).astype(BF16)
    oh = oh_ref[...]
    o_ref[...] = _dot(hi, oh) + (_dot(mid, oh) + _dot(lo, oh))


def _t5_bucket(rel):
    n = -rel
    half = N_BUCKETS // 2
    ret = jnp.where(n < 0, half, 0)
    n = jnp.abs(n)
    max_exact = half // 2
    large = max_exact + (jnp.log(jnp.maximum(n, 1).astype(F32) / max_exact)
                         / math.log(MAX_DISTANCE / max_exact) * (half - max_exact)).astype(jnp.int32)
    large = jnp.minimum(large, half - 1)
    return ret + jnp.where(n < max_exact, n, large)


def _swa_bias(rel_bias, q_len, prev_len):
    klen = prev_len + q_len
    rel = jnp.arange(klen)[None, :] - prev_len - jnp.arange(q_len)[:, None]
    bucket = _t5_bucket(rel).reshape(1, q_len * klen)
    onehot = (bucket == jnp.arange(N_BUCKETS)[:, None]).astype(BF16)
    tab = pl.pallas_call(
        _bias_kernel,
        out_shape=jax.ShapeDtypeStruct((A_HEADS, q_len * klen), F32),
        name="t5_bias",
    )(rel_bias.T.astype(F32), onehot)
    return tab.reshape(A_KV, A_GROUP * q_len, klen)


def _inproj_kernel(x_ref, g_ref, sc_ref, sh_ref, wm_ref, wg_ref, wa_ref, gb_ref,
                   qm_ref, km_ref, vm_ref, om_ref, gt_ref, qa_ref, ka_ref, va_ref):
    x = x_ref[...]
    h = _rms(x) * g_ref[...] * (1.0 + sc_ref[0]) + sh_ref[0]
    hb = h.astype(BF16)
    hl = (h - hb.astype(F32)).astype(BF16)
    zm = _dot(hb, wm_ref[...])
    qm_ref[...] = zm[:, :M_WIDTH]
    km_ref[...] = zm[:, M_WIDTH:2 * M_WIDTH] * (M_HD ** -0.5)
    vm_ref[...] = zm[:, 2 * M_WIDTH:3 * M_WIDTH]
    om_ref[...] = zm[:, 3 * M_WIDTH:]
    wg = wg_ref[...]
    r = _dot(hb, wg) + _dot(hl, wg)
    gt_ref[...] = r[:, :GATE_PAD] + r[:, GATE_PAD:] + gb_ref[...]
    za = _dot(hb, wa_ref[...])
    qa_ref[...] = za[:, :A_WIDTH]
    ka_ref[...] = za[:, A_WIDTH:A_WIDTH + A_KVW]
    va_ref[...] = za[:, A_WIDTH + A_KVW:]


def _mod_blocks(vec, seq, tile):
    b, d = vec.shape
    if seq % tile == 0:
        return jnp.repeat(vec, seq // tile, axis=0).reshape(b * (seq // tile), 1, d)
    assert tile % seq == 0 and (b * seq) % tile == 0
    return jnp.repeat(vec, seq, axis=0).reshape(b * seq // tile, tile, d)


def _inproj(xf, g, sc, sh, wm, wg, wa, gb):
    n, d = xf.shape
    t = TOK_TILE
    r = sc.shape[1]
    row = lambda w: pl.BlockSpec((t, w), lambda i: (i, 0))
    full = lambda a: pl.BlockSpec(a.shape, lambda i: (0,) * a.ndim)
    mod = pl.BlockSpec((1, r, d), lambda i: (i, 0, 0))
    widths = (M_WIDTH, M_WIDTH, M_WIDTH, M_WIDTH, GATE_PAD, A_WIDTH, A_KVW, A_KVW)
    return pl.pallas_call(
        _inproj_kernel,
        grid=(n // t,),
        in_specs=[row(d), full(g), mod, mod, full(wm), full(wg), full(wa), full(gb)],
        out_specs=[row(w) for w in widths],
        out_shape=[jax.ShapeDtypeStruct((n, w), F32) for w in widths],
        compiler_params=_params("arbitrary"),
        name="inproj",
    )(xf, g, sc, sh, wm, wg, wa, gb)


def _mlstm_kernel(q_ref, k_ref, v_ref, o_ref, g_ref, gt_ref, c0_ref, n0_ref, m0_ref, mn_ref,
                  h_ref, c1_ref, n1_ref, m1_ref, c_s, n_s, m_s):
    j = pl.program_id(1)
    L = q_ref.shape[1]

    @pl.when(j == 0)
    def _():
        c_s[...] = c0_ref[0]
        n_s[...] = n0_ref[0]
        m_s[...] = m0_ref[0]

    row = lax.broadcasted_iota(jnp.int32, (L, L), 0)
    col = lax.broadcasted_iota(jnp.int32, (L, L), 1)
    causal = row >= col
    for h in range(M_HEADS):
        sl = slice(h * M_HD, (h + 1) * M_HD)
        q = q_ref[0, :, sl]
        k = k_ref[0, :, sl]
        v = v_ref[0, :, sl]
        qb, kb, vb = q.astype(BF16), k.astype(BF16), v.astype(BF16)
        ig_c = g_ref[0, :, h:h + 1]
        lf_c = _log_sigmoid(g_ref[0, :, M_HEADS + h:M_HEADS + h + 1])
        ig_r = gt_ref[0, h:h + 1, :]
        lf_r = _log_sigmoid(gt_ref[0, M_HEADS + h:M_HEADS + h + 1, :])
        b_c = jnp.sum(jnp.where(causal, lf_r, 0.0), axis=1, keepdims=True)
        b_r = jnp.sum(jnp.where(row <= col, lf_c, 0.0), axis=0, keepdims=True)
        c0 = c_s[h]
        n0 = n_s[h:h + 1, :]
        m0 = m_s[h:h + 1, 0:1]
        log_d = jnp.where(causal, b_c - b_r + ig_r, NEG)
        m_inter = b_c + m0
        m_t = jnp.maximum(m_inter, jnp.max(log_d, axis=1, keepdims=True))
        s = _dot_nt(qb, kb) * jnp.exp(log_d - m_t)
        a = jnp.exp(m_inter - m_t)
        num = a * _dot_nt(qb, c0.astype(BF16)) + _dot(s.astype(BF16), vb)
        den = a * jnp.sum(q * n0, axis=1, keepdims=True) + jnp.sum(s, axis=1, keepdims=True)
        hh = num / jnp.maximum(jnp.abs(den), jnp.exp(-m_t))
        b_end = b_r[:, L - 1:L]
        g_r = b_end - b_r + ig_r
        m_new = jnp.maximum(b_end + m0, jnp.max(g_r, axis=1, keepdims=True))
        w_c = jnp.exp(b_end - b_c + ig_c - m_new)
        a_end = jnp.exp(b_end + m0 - m_new)
        c_s[h] = a_end * c0 + _dot_tn((v * w_c).astype(BF16), kb)
        n_s[h:h + 1, :] = a_end * n0 + jnp.sum(k * w_c, axis=0, keepdims=True)
        m_s[h:h + 1, :] = jnp.broadcast_to(m_new, (1, M_HD))
        hn = _rms(hh) * mn_ref[:, sl]
        h_ref[0, :, sl] = hn * jax.nn.sigmoid(o_ref[0, :, sl])

    @pl.when(j == pl.num_programs(1) - 1)
    def _():
        c1_ref[0] = c_s[...]
        n1_ref[0] = n_s[...]
        m1_ref[0] = m_s[...]


def _mlstm(qm, km, vm, om, gt, gtt, c0, n0, m0, mn, chunk):
    b, s, _ = qm.shape
    nc = s // chunk
    seq = lambda w: pl.BlockSpec((1, chunk, w), lambda i, j: (i, j, 0))
    st4 = pl.BlockSpec((1, M_HEADS, M_HD, M_HD), lambda i, j: (i, 0, 0, 0))
    st3 = pl.BlockSpec((1, M_HEADS, M_HD), lambda i, j: (i, 0, 0))
    return pl.pallas_call(
        _mlstm_kernel,
        grid=(b, nc),
        in_specs=[seq(M_WIDTH), seq(M_WIDTH), seq(M_WIDTH), seq(M_WIDTH), seq(GATE_PAD),
                  pl.BlockSpec((1, 2 * M_HEADS, chunk), lambda i, j: (i, 0, j)),
                  st4, st3, st3, pl.BlockSpec((1, M_WIDTH), lambda i, j: (0, 0))],
        out_specs=[seq(M_WIDTH), st4, st3, st3],
        out_shape=[jax.ShapeDtypeStruct((b, s, M_WIDTH), F32),
                   jax.ShapeDtypeStruct((b, M_HEADS, M_HD, M_HD), F32),
                   jax.ShapeDtypeStruct((b, M_HEADS, M_HD), F32),
                   jax.ShapeDtypeStruct((b, M_HEADS, M_HD), F32)],
        scratch_shapes=[pltpu.VMEM((M_HEADS, M_HD, M_HD), F32),
                        pltpu.VMEM((M_HEADS, M_HD), F32),
                        pltpu.VMEM((M_HEADS, M_HD), F32)],
        compiler_params=_params("arbitrary", "arbitrary"),
        name="mlstm",
    )(qm, km, vm, om, gt, gtt, c0, n0, m0, mn)


def _swa_kernel(q_ref, kp_ref, ko_ref, vp_ref, vo_ref, bias_ref, sink_ref, o_ref, kc_s, vc_s,
                *, lc, mask_first):
    i = pl.program_id(1)
    qb_rows = q_ref.shape[1]
    p = kp_ref.shape[1]
    klen = p + lc
    kc_s[0:p, :] = kp_ref[0].astype(BF16)
    kc_s[p:, :] = ko_ref[0].astype(BF16)
    vc_s[0:p, :] = vp_ref[0].astype(BF16)
    vc_s[p:, :] = vo_ref[0].astype(BF16)
    for c in range(qb_rows // lc):
        r0 = c * lc
        for kv in range(A_KV):
            hs = [(kv * A_GROUP + g) * A_HD for g in range(A_GROUP)]
            qs = jnp.concatenate([q_ref[0, r0:r0 + lc, h0:h0 + A_HD] for h0 in hs], axis=0).astype(BF16)
            ks = kc_s[r0:r0 + klen, kv * A_HD:(kv + 1) * A_HD]
            vs = vc_s[r0:r0 + klen, kv * A_HD:(kv + 1) * A_HD]
            sc = _dot_nt(qs, ks) * (A_HD ** -0.5) + bias_ref[kv]
            if mask_first and r0 < p:
                colv = lax.broadcasted_iota(jnp.int32, sc.shape, 1) + r0
                sc = jnp.where((colv >= p) | (i > 0), sc, NEG)
            sk = sink_ref[kv][:, 0:1]
            mx = jnp.maximum(jnp.max(sc, axis=1, keepdims=True), sk)
            pr = jnp.exp(sc - mx)
            den = jnp.sum(pr, axis=1, keepdims=True) + jnp.exp(sk - mx)
            o = _dot(pr.astype(BF16), vs) / den
            for g, h0 in enumerate(hs):
                o_ref[0, r0:r0 + lc, h0:h0 + A_HD] = o[g * lc:(g + 1) * lc]


def _swa(qa, k_prev, k_own, v_prev, v_own, bias, sink_tab, qb, lc, prev_map, mask_first):
    b, s, _ = qa.shape
    p = WINDOW
    own = lambda w: pl.BlockSpec((1, qb, w), lambda i, j: (i, j, 0))
    prev = pl.BlockSpec((1, p, A_KVW), prev_map)
    full = lambda a: pl.BlockSpec(a.shape, lambda i, j: (0,) * a.ndim)
    return pl.pallas_call(
        functools.partial(_swa_kernel, lc=lc, mask_first=mask_first),
        grid=(b, s // qb),
        in_specs=[own(A_WIDTH), prev, own(A_KVW), prev, own(A_KVW), full(bias), full(sink_tab)],
        out_specs=own(A_WIDTH),
        out_shape=jax.ShapeDtypeStruct((b, s, A_WIDTH), F32),
        scratch_shapes=[pltpu.VMEM((p + qb, A_KVW), BF16), pltpu.VMEM((p + qb, A_KVW), BF16)],
        compiler_params=_params("arbitrary", "arbitrary"),
        name="swa",
    )(qa, k_prev, k_own, v_prev, v_own, bias, sink_tab)


def _outproj_kernel(hm_ref, ha_ref, x_ref, gate_ref, wom_ref, woa_ref, g2_ref, sc_ref, sh_ref, wq_ref,
                    x1_ref, h2_ref, q_ref):
    y = _dot(hm_ref[...].astype(BF16), wom_ref[...]) + _dot(ha_ref[...].astype(BF16), woa_ref[...])
    x1 = x_ref[...] + gate_ref[0] * y
    x1_ref[...] = x1
    h2 = (_rms(x1) * g2_ref[...] * (1.0 + sc_ref[0]) + sh_ref[0]).astype(BF16)
    h2_ref[...] = h2
    q_ref[...] = _dot(h2, wq_ref[...]).astype(BF16)


def _outproj(hm, ha, xf, gate, wom, woa, g2, sc, sh, wq):
    n, d = xf.shape
    t = TOK_TILE
    r = sc.shape[1]
    row = lambda w: pl.BlockSpec((t, w), lambda i: (i, 0))
    full = lambda a: pl.BlockSpec(a.shape, lambda i: (0,) * a.ndim)
    mod = pl.BlockSpec((1, r, d), lambda i: (i, 0, 0))
    nq = wq.shape[1]
    return pl.pallas_call(
        _outproj_kernel,
        grid=(n // t,),
        in_specs=[row(M_WIDTH), row(A_WIDTH), row(d), mod, full(wom), full(woa), full(g2), mod, mod, full(wq)],
        out_specs=[row(d), row(d), row(nq)],
        out_shape=[jax.ShapeDtypeStruct((n, d), F32), jax.ShapeDtypeStruct((n, d), BF16),
                   jax.ShapeDtypeStruct((n, nq), BF16)],
        compiler_params=_params("arbitrary"),
        name="outproj",
    )(hm, ha, xf, gate, wom, woa, g2, sc, sh, wq)


def _top_desc(s, k):
    vals = []
    cur = s
    for _ in range(k):
        m = jnp.max(cur, axis=0, keepdims=True)
        vals.append(m)
        cur = jnp.where(cur == m, NEG, cur)
    return jnp.concatenate(vals, axis=0)


def _peer_kernel(h_ref, q_ref, k1_ref, k2_ref, u_ref, vt_ref, x_ref, gate_ref, fg_ref, o_ref,
                 s1_s, s2_s, e1_s, e2_s, tau_s, w_s, acc_s, *, final):
    e = pl.program_id(1)
    tt = h_ref.shape[0]
    eb = u_ref.shape[0]
    groups = eb // N_KEYS

    @pl.when(e == 0)
    def _():
        for h in range(PEER_HEADS):
            c0 = h * 2 * PEER_DKH
            s1 = _dot_nt(k1_ref[h], q_ref[:, c0:c0 + PEER_DKH])
            s2 = _dot_nt(k2_ref[h], q_ref[:, c0 + PEER_DKH:c0 + 2 * PEER_DKH])
            a = _top_desc(s1, PEER_TOPK)
            b = _top_desc(s2, PEER_TOPK)
            cand = jnp.concatenate([a[p:p + 1] + b for p in range(PEER_TOPK)], axis=0)
            tau = _top_desc(cand, PEER_TOPK)[PEER_TOPK - 1:PEER_TOPK]
            mx = a[0:1] + b[0:1]
            z = jnp.sum(jnp.where(cand >= tau, jnp.exp(cand - mx), 0.0), axis=0, keepdims=True)
            s1_s[h] = s1
            s2_s[h] = s2
            e1_s[h] = jnp.exp(s1 - a[0:1]) / z
            e2_s[h] = jnp.exp(s2 - b[0:1])
            tau_s[h] = jnp.broadcast_to(tau, (PEER_ROWS, tt))
        acc_s[...] = jnp.zeros_like(acc_s)

    act = jax.nn.gelu(_dot_nt(u_ref[...], h_ref[...]), approximate=True)

    def group_body(ii, carry):
        i = e * groups + ii

        def rows_body(r, carry2):
            r0 = pl.multiple_of(r * PEER_ROWS, PEER_ROWS)
            g = jnp.zeros((PEER_ROWS, tt), F32)
            for h in range(PEER_HEADS):
                s1r = s1_s[h, pl.ds(i, 1), :]
                e1r = e1_s[h, pl.ds(i, 1), :]
                sel = (s1r + s2_s[h, pl.ds(r0, PEER_ROWS), :]) >= tau_s[h]
                g = g + jnp.where(sel, e2_s[h, pl.ds(r0, PEER_ROWS), :], 0.0) * e1r
            w_s[pl.ds(pl.multiple_of(ii * N_KEYS + r0, PEER_ROWS), PEER_ROWS), :] = g
            return carry2

        return lax.fori_loop(0, N_KEYS // PEER_ROWS, rows_body, carry)

    lax.fori_loop(0, groups, group_body, 0)
    w = (w_s[...] * act).astype(BF16)
    acc_s[...] += _dot(vt_ref[...], w)

    @pl.when(e == pl.num_programs(1) - 1)
    def _():
        x2 = x_ref[...] + gate_ref[0] * acc_s[...].T
        if final:
            x2 = _rms(x2) * fg_ref[...]
        o_ref[...] = x2


def _peer(h2, q, k1, k2, u, vt, x1, gate, fg, final):
    n, d = x1.shape
    t = PEER_TOK
    eb = PEER_EB
    ne = u.shape[0]
    r = gate.shape[1]
    row = lambda w: pl.BlockSpec((t, w), lambda i, e: (i, 0))
    full = lambda a: pl.BlockSpec(a.shape, lambda i, e: (0,) * a.ndim)
    head_tab = pltpu.VMEM((PEER_HEADS, N_KEYS, t), F32)
    return pl.pallas_call(
        functools.partial(_peer_kernel, final=final),
        grid=(n // t, ne // eb),
        in_specs=[row(d), row(q.shape[1]), full(k1), full(k2),
                  pl.BlockSpec((eb, d), lambda i, e: (e, 0)),
                  pl.BlockSpec((d, eb), lambda i, e: (0, e)),
                  row(d), pl.BlockSpec((1, r, d), lambda i, e: (i, 0, 0)), full(fg)],
        out_specs=row(d),
        out_shape=jax.ShapeDtypeStruct((n, d), F32),
        scratch_shapes=[head_tab, head_tab, head_tab, head_tab,
                        pltpu.VMEM((PEER_HEADS, PEER_ROWS, t), F32),
                        pltpu.VMEM((eb, t), F32),
                        pltpu.VMEM((d, t), F32)],
        compiler_params=_params("arbitrary", "arbitrary"),
        name="peer",
    )(h2, q, k1, k2, u, vt, x1, gate, fg)


def _layer_weights(l, norm1_g, norm2_g, w_in, b_igate, b_fgate, mnorm_g, sinks, w_out,
                   peer_wq, peer_k1, peer_k2, peer_u, peer_v):
    d = w_in.shape[1]
    w = w_in[l]
    m4 = 4 * M_WIDTH
    wgate = w[:, m4:m4 + 2 * M_HEADS]
    ghi = wgate.astype(BF16)
    glo = (wgate - ghi.astype(F32)).astype(BF16)
    wg = jnp.zeros((d, 2 * GATE_PAD), BF16)
    wg = wg.at[:, :2 * M_HEADS].set(ghi).at[:, GATE_PAD:GATE_PAD + 2 * M_HEADS].set(glo)
    gb = jnp.zeros((1, GATE_PAD), F32).at[0, :M_HEADS].set(b_igate[l]).at[0, M_HEADS:2 * M_HEADS].set(b_fgate[l])
    return dict(
        g1=norm1_g[l].reshape(1, d), g2=norm2_g[l].reshape(1, d),
        wm=w[:, :m4].astype(BF16), wg=wg, wa=w[:, m4 + 2 * M_HEADS:].astype(BF16), gb=gb,
        mn=mnorm_g[l].reshape(1, M_WIDTH), sinks=sinks[l],
        wom=w_out[l][:M_WIDTH].astype(BF16), woa=w_out[l][M_WIDTH:].astype(BF16),
        wq=peer_wq[l].astype(BF16), k1=peer_k1[l].astype(BF16), k2=peer_k2[l].astype(BF16),
        u=peer_u[l].astype(BF16), vt=peer_v[l].T.astype(BF16))


def _trunk(x, mods, layers, bias, fg, st_c, st_n, st_m, cache_k, cache_v, prompt):
    b, s, d = x.shape
    n = b * s
    xf = x.reshape(n, d)
    lc = CHUNK if prompt else s
    chunk = min(MLSTM_CHUNK, s)
    qb = min(SWA_QB, s)
    outs = []
    for l, lw in enumerate(layers):
        sh1, sc1, gt1, sh2, sc2, gt2 = jnp.split(mods[l], 6, axis=-1)
        tmod = lambda v: _mod_blocks(v, s, TOK_TILE)
        qm, km, vm, om, gts, qa, ka, va = _inproj(xf, lw["g1"], tmod(sc1), tmod(sh1),
                                                  lw["wm"], lw["wg"], lw["wa"], lw["gb"])
        seq = lambda a: a.reshape(b, s, a.shape[-1])
        gts3 = seq(gts)
        gtt = jnp.swapaxes(gts3[:, :, :2 * M_HEADS], 1, 2)
        if prompt:
            c0 = jnp.zeros((b, M_HEADS, M_HD, M_HD), F32)
            n0 = jnp.zeros((b, M_HEADS, M_HD), F32)
            m0 = jnp.zeros((b, M_HEADS, M_HD), F32)
        else:
            c0, n0 = st_c[l].astype(F32), st_n[l].astype(F32)
            m0 = jnp.broadcast_to(st_m[l].astype(F32)[:, :, None], (b, M_HEADS, M_HD))
        hm, c1, n1, m1 = _mlstm(seq(qm), seq(km), seq(vm), seq(om), gts3, gtt, c0, n0, m0, lw["mn"], chunk)
        ka3, va3 = seq(ka), seq(va)
        sink_tab = jnp.broadcast_to(
            jnp.repeat(lw["sinks"].astype(F32).reshape(A_KV, A_GROUP), lc, axis=1)[:, :, None],
            (A_KV, A_GROUP * lc, 128))
        if prompt:
            per = qb // WINDOW
            prev_map = lambda i, j: (i, jnp.maximum(j * per - 1, 0), 0)
            ha = _swa(seq(qa), ka3, ka3, va3, va3, bias, sink_tab, qb, lc, prev_map, True)
            new_k, new_v = ka3[:, -WINDOW:], va3[:, -WINDOW:]
        else:
            ck = cache_k[l].astype(F32).reshape(b, -1, A_KVW)
            cv = cache_v[l].astype(F32).reshape(b, -1, A_KVW)
            ha = _swa(seq(qa), ck, ka3, cv, va3, bias, sink_tab, qb, lc, lambda i, j: (i, 0, 0), False)
            pl_ = ck.shape[1]
            new_k = jnp.concatenate([ck, ka3], axis=1)[:, -pl_:]
            new_v = jnp.concatenate([cv, va3], axis=1)[:, -pl_:]
        x1, h2, q = _outproj(hm.reshape(n, M_WIDTH), ha.reshape(n, A_WIDTH), xf, tmod(gt1),
                             lw["wom"], lw["woa"], lw["g2"], tmod(sc2), tmod(sh2), lw["wq"])
        final = l == len(layers) - 1
        xf = _peer(h2, q, lw["k1"], lw["k2"], lw["u"], lw["vt"], x1, _mod_blocks(gt2, s, PEER_TOK), fg, final)
        outs.append((c1, n1, m1[:, :, 0], new_k.reshape(b, -1, A_KV, A_HD), new_v.reshape(b, -1, A_KV, A_HD)))
    stack = lambda k: jnp.stack([o[k] for o in outs])
    return (xf.reshape(b, s, d), stack(0), stack(1), stack(2), stack(3), stack(4))


def kernel(x_prompt, x_sample, cache_k, cache_v, state_C, state_n, state_m, c_prompt, c_sample, w_ada, b_ada,
           norm1_g, norm2_g, w_in, b_igate, b_fgate, mnorm_g, sinks, rel_bias, w_out, peer_wq, peer_k1, peer_k2,
           peer_u, peer_v, final_g):
    depth = w_in.shape[0]
    d = x_prompt.shape[-1]
    bp, bs = c_prompt.shape[0], c_sample.shape[0]
    c_all = jnp.concatenate([c_prompt, c_sample], axis=0).astype(F32)
    pad = -c_all.shape[0] % 8
    c_all = jnp.pad(c_all, ((0, pad), (0, 0)))
    mods = _ada(c_all, w_ada, b_ada)
    layers = [_layer_weights(l, norm1_g, norm2_g, w_in, b_igate, b_fgate, mnorm_g, sinks, w_out,
                             peer_wq, peer_k1, peer_k2, peer_u, peer_v) for l in range(depth)]
    fg = final_g.reshape(1, d)
    s_len = x_sample.shape[1]
    bias_p = _swa_bias(rel_bias, CHUNK, WINDOW)
    bias_s = _swa_bias(rel_bias, s_len, cache_k.shape[2])
    yp, pc, pn, pm, pk, pv = _trunk(x_prompt, mods[:, :bp], layers, bias_p, fg, None, None, None, None, None, True)
    ys, sc, sn, sm, sk, sv = _trunk(x_sample, mods[:, bp:bp + bs], layers, bias_s, fg,
                                    state_C, state_n, state_m, cache_k, cache_v, False)
    return (yp, ys, pc, pn, pm, pk, pv, sc, sn, sm, sk, sv)
```

```python
import functools
import math

import jax
import jax.numpy as jnp
from jax import lax
from jax.experimental import pallas as pl
from jax.experimental.pallas import tpu as pltpu

F32, BF16 = jnp.float32, jnp.bfloat16
EPS = 1e-6
NEG = -1e30

M_HEADS, M_HD = 4, 128
M_WIDTH = M_HEADS * M_HD
A_HEADS, A_KV, A_HD = 8, 2, 64
A_GROUP = A_HEADS // A_KV
A_WIDTH = A_HEADS * A_HD
A_KVW = A_KV * A_HD
CHUNK, WINDOW = 64, 128
N_BUCKETS, MAX_DISTANCE = 32, 128
PEER_HEADS, N_KEYS, PEER_TOPK, PEER_DKH = 8, 128, 16, 128
GATE_PAD = 128

TOK_TILE = 256
PEER_TOK = 512
PEER_EB = 1024
PEER_ROWS = 16
MLSTM_CHUNK = 256
SWA_QB = 256
VMEM_LIMIT = 56 * 1024 * 1024


def _params(*sem):
    return pltpu.CompilerParams(dimension_semantics=sem, vmem_limit_bytes=VMEM_LIMIT)


def _dot(a, b):
    return jnp.dot(a, b, preferred_element_type=F32)


def _dot_nt(a, b):
    return lax.dot_general(a, b, (((1,), (1,)), ((), ())), preferred_element_type=F32)


def _dot_tn(a, b):
    return lax.dot_general(a, b, (((0,), (0,)), ((), ())), preferred_element_type=F32)


def _split2(x):
    hi = x.astype(BF16)
    lo = (x - hi.astype(F32)).astype(BF16)
    return hi, lo


def _dot3(a, w):
    ah, al = _split2(a)
    wh, wl = _split2(w)
    return _dot(ah, wh) + (_dot(ah, wl) + _dot(al, wh))


def _rms(x):
    return x * lax.rsqrt(jnp.mean(x * x, axis=-1, keepdims=True) + EPS)


def _log_sigmoid(x):
    return jnp.minimum(x, 0.0) - jnp.log1p(jnp.exp(-jnp.abs(x)))


def _ada_kernel(c_ref, w_ref, b_ref, o_ref):
    c = c_ref[...]
    cs = c * jax.nn.sigmoid(c)
    o_ref[0] = _dot3(cs, w_ref[0]) + b_ref[0]


def _ada(c_all, w_ada, b_ada):
    depth, d, n6 = w_ada.shape
    m = c_all.shape[0]
    bn = 1024
    return pl.pallas_call(
        _ada_kernel,
        grid=(depth, n6 // bn),
        in_specs=[pl.BlockSpec((m, d), lambda l, j: (0, 0)),
                  pl.BlockSpec((1, d, bn), lambda l, j: (l, 0, j)),
                  pl.BlockSpec((1, 1, bn), lambda l, j: (l, 0, j))],
        out_specs=pl.BlockSpec((1, m, bn), lambda l, j: (l, 0, j)),
        out_shape=jax.ShapeDtypeStruct((depth, m, n6), F32),
        compiler_params=_params("arbitrary", "arbitrary"),
        name="adaln",
    )(c_all, w_ada, b_ada.reshape(depth, 1, n6))


def _bias_kernel(rb_ref, oh_ref, o_ref):
    x = rb_ref[...]
    hi = x.astype(BF16)
    r1 = x - hi.astype(F32)
    mid = r1.astype(BF16)
    lo = (r1 - mid.astype(F32)).astype(BF16)
    oh = oh_ref[...]
    o_ref[...] = _dot(hi, oh) + (_dot(mid, oh) + _dot(lo, oh))


def _t5_bucket(rel):
    n = -rel
    half = N_BUCKETS // 2
    ret = jnp.where(n < 0, half, 0)
    n = jnp.abs(n)
    max_exact = half // 2
    large = max_exact + (jnp.log(jnp.maximum(n, 1).astype(F32) / max_exact)
                         / math.log(MAX_DISTANCE / max_exact) * (half - max_exact)).astype(jnp.int32)
    large = jnp.minimum(large, half - 1)
    return ret + jnp.where(n < max_exact, n, large)


def _swa_bias(rel_bias, q_len, prev_len):
    klen = prev_len + q_len
    rel = jnp.arange(klen)[None, :] - prev_len - jnp.arange(q_len)[:, None]
    bucket = _t5_bucket(rel).reshape(1, q_len * klen)
    onehot = (bucket == jnp.arange(N_BUCKETS)[:, None]).astype(BF16)
    tab = pl.pallas_call(
        _bias_kernel,
        out_shape=jax.ShapeDtypeStruct((A_HEADS, q_len * klen), F32),
        name="t5_bias",
    )(rel_bias.T.astype(F32), onehot)
    return tab.reshape(A_KV, A_GROUP * q_len, klen)


def _inproj_kernel(x_ref, g_ref, sc_ref, sh_ref, wm_ref, wg_ref, wa_ref, gb_ref,
                   qm_ref, km_ref, vm_ref, om_ref, gt_ref, qa_ref, ka_ref, va_ref):
    x = x_ref[...]
    h = _rms(x) * g_ref[...] * (1.0 + sc_ref[0]) + sh_ref[0]
    hb = h.astype(BF16)
    hl = (h - hb.astype(F32)).astype(BF16)
    zm = _dot(hb, wm_ref[...])
    qm_ref[...] = zm[:, :M_WIDTH]
    km_ref[...] = zm[:, M_WIDTH:2 * M_WIDTH] * (M_HD ** -0.5)
    vm_ref[...] = zm[:, 2 * M_WIDTH:3 * M_WIDTH]
    om_ref[...] = zm[:, 3 * M_WIDTH:]
    wg = wg_ref[...]
    r = _dot(hb, wg) + _dot(hl, wg)
    gt_ref[...] = r[:, :GATE_PAD] + r[:, GATE_PAD:] + gb_ref[...]
    za = _dot(hb, wa_ref[...])
    qa_ref[...] = za[:, :A_WIDTH]
    ka_ref[...] = za[:, A_WIDTH:A_WIDTH + A_KVW]
    va_ref[...] = za[:, A_WIDTH + A_KVW:]


def _mod_blocks(vec, seq, tile):
    b, d = vec.shape
    if seq % tile == 0:
        return jnp.repeat(vec, seq // tile, axis=0).reshape(b * (seq // tile), 1, d)
    assert tile % seq == 0 and (b * seq) % tile == 0
    return jnp.repeat(vec, seq, axis=0).reshape(b * seq // tile, tile, d)


def _inproj(xf, g, sc, sh, wm, wg, wa, gb):
    n, d = xf.shape
    t = TOK_TILE
    r = sc.shape[1]
    row = lambda w: pl.BlockSpec((t, w), lambda i: (i, 0))
    full = lambda a: pl.BlockSpec(a.shape, lambda i: (0,) * a.ndim)
    mod = pl.BlockSpec((1, r, d), lambda i: (i, 0, 0))
    widths = (M_WIDTH, M_WIDTH, M_WIDTH, M_WIDTH, GATE_PAD, A_WIDTH, A_KVW, A_KVW)
    return pl.pallas_call(
        _inproj_kernel,
        grid=(n // t,),
        in_specs=[row(d), full(g), mod, mod, full(wm), full(wg), full(wa), full(gb)],
        out_specs=[row(w) for w in widths],
        out_shape=[jax.ShapeDtypeStruct((n, w), F32) for w in widths],
        compiler_params=_params("arbitrary"),
        name="inproj",
    )(xf, g, sc, sh, wm, wg, wa, gb)


def _mlstm_kernel(q_ref, k_ref, v_ref, o_ref, g_ref, gt_ref, c0_ref, n0_ref, m0_ref, mn_ref,
                  h_ref, c1_ref, n1_ref, m1_ref, c_s, n_s, m_s):
    j = pl.program_id(1)
    L = q_ref.shape[1]

    @pl.when(j == 0)
    def _():
        c_s[...] = c0_ref[0]
        n_s[...] = n0_ref[0]
        m_s[...] = m0_ref[0]

    row = lax.broadcasted_iota(jnp.int32, (L, L), 0)
    col = lax.broadcasted_iota(jnp.int32, (L, L), 1)
    causal = row >= col
    for h in range(M_HEADS):
        sl = slice(h * M_HD, (h + 1) * M_HD)
        q = q_ref[0, :, sl]
        k = k_ref[0, :, sl]
        v = v_ref[0, :, sl]
        qb, kb, vb = q.astype(BF16), k.astype(BF16), v.astype(BF16)
        ig_c = g_ref[0, :, h:h + 1]
        lf_c = _log_sigmoid(g_ref[0, :, M_HEADS + h:M_HEADS + h + 1])
        ig_r = gt_ref[0, h:h + 1, :]
        lf_r = _log_sigmoid(gt_ref[0, M_HEADS + h:M_HEADS + h + 1, :])
        b_c = jnp.sum(jnp.where(causal, lf_r, 0.0), axis=1, keepdims=True)
        b_r = jnp.sum(jnp.where(row <= col, lf_c, 0.0), axis=0, keepdims=True)
        c0 = c_s[h]
        n0 = n_s[h:h + 1, :]
        m0 = m_s[h:h + 1, 0:1]
        log_d = jnp.where(causal, b_c - b_r + ig_r, NEG)
        m_inter = b_c + m0
        m_t = jnp.maximum(m_inter, jnp.max(log_d, axis=1, keepdims=True))
        s = _dot_nt(qb, kb) * jnp.exp(log_d - m_t)
        a = jnp.exp(m_inter - m_t)
        num = a * _dot_nt(qb, c0.astype(BF16)) + _dot(s.astype(BF16), vb)
        den = a * jnp.sum(q * n0, axis=1, keepdims=True) + jnp.sum(s, axis=1, keepdims=True)
        hh = num / jnp.maximum(jnp.abs(den), jnp.exp(-m_t))
        b_end = b_r[:, L - 1:L]
        g_r = b_end - b_r + ig_r
        m_new = jnp.maximum(b_end + m0, jnp.max(g_r, axis=1, keepdims=True))
        w_c = jnp.exp(b_end - b_c + ig_c - m_new)
        a_end = jnp.exp(b_end + m0 - m_new)
        c_s[h] = a_end * c0 + _dot_tn((v * w_c).astype(BF16), kb)
        n_s[h:h + 1, :] = a_end * n0 + jnp.sum(k * w_c, axis=0, keepdims=True)
        m_s[h:h + 1, :] = jnp.broadcast_to(m_new, (1, M_HD))
        hn = _rms(hh) * mn_ref[:, sl]
        h_ref[0, :, sl] = hn * jax.nn.sigmoid(o_ref[0, :, sl])

    @pl.when(j == pl.num_programs(1) - 1)
    def _():
        c1_ref[0] = c_s[...]
        n1_ref[0] = n_s[...]
        m1_ref[0] = m_s[...]


def _mlstm(qm, km, vm, om, gt, gtt, c0, n0, m0, mn, chunk):
    b, s, _ = qm.shape
    nc = s // chunk
    seq = lambda w: pl.BlockSpec((1, chunk, w), lambda i, j: (i, j, 0))
    st4 = pl.BlockSpec((1, M_HEADS, M_HD, M_HD), lambda i, j: (i, 0, 0, 0))
    st3 = pl.BlockSpec((1, M_HEADS, M_HD), lambda i, j: (i, 0, 0))
    return pl.pallas_call(
        _mlstm_kernel,
        grid=(b, nc),
        in_specs=[seq(M_WIDTH), seq(M_WIDTH), seq(M_WIDTH), seq(M_WIDTH), seq(GATE_PAD),
                  pl.BlockSpec((1, 2 * M_HEADS, chunk), lambda i, j: (i, 0, j)),
                  st4, st3, st3, pl.BlockSpec((1, M_WIDTH), lambda i, j: (0, 0))],
        out_specs=[seq(M_WIDTH), st4, st3, st3],
        out_shape=[jax.ShapeDtypeStruct((b, s, M_WIDTH), F32),
                   jax.ShapeDtypeStruct((b, M_HEADS, M_HD, M_HD), F32),
                   jax.ShapeDtypeStruct((b, M_HEADS, M_HD), F32),
                   jax.ShapeDtypeStruct((b, M_HEADS, M_HD), F32)],
        scratch_shapes=[pltpu.VMEM((M_HEADS, M_HD, M_HD), F32),
                        pltpu.VMEM((M_HEADS, M_HD), F32),
                        pltpu.VMEM((M_HEADS, M_HD), F32)],
        compiler_params=_params("arbitrary", "arbitrary"),
        name="mlstm",
    )(qm, km, vm, om, gt, gtt, c0, n0, m0, mn)


def _swa_kernel(q_ref, kp_ref, ko_ref, vp_ref, vo_ref, bias_ref, sink_ref, o_ref, kc_s, vc_s,
                *, lc, mask_first):
    i = pl.program_id(1)
    qb_rows = q_ref.shape[1]
    p = kp_ref.shape[1]
    klen = p + lc
    kc_s[0:p, :] = kp_ref[0].astype(BF16)
    kc_s[p:, :] = ko_ref[0].astype(BF16)
    vc_s[0:p, :] = vp_ref[0].astype(BF16)
    vc_s[p:, :] = vo_ref[0].astype(BF16)
    for c in range(qb_rows // lc):
        r0 = c * lc
        for kv in range(A_KV):
            hs = [(kv * A_GROUP + g) * A_HD for g in range(A_GROUP)]
            qs = jnp.concatenate([q_ref[0, r0:r0 + lc, h0:h0 + A_HD] for h0 in hs], axis=0).astype(BF16)
            ks = kc_s[r0:r0 + klen, kv * A_HD:(kv + 1) * A_HD]
            vs = vc_s[r0:r0 + klen, kv * A_HD:(kv + 1) * A_HD]
            sc = _dot_nt(qs, ks) * (A_HD ** -0.5) + bias_ref[kv]
            if mask_first and r0 < p:
                colv = lax.broadcasted_iota(jnp.int32, sc.shape, 1) + r0
                sc = jnp.where((colv >= p) | (i > 0), sc, NEG)
            sk = sink_ref[kv][:, 0:1]
            mx = jnp.maximum(jnp.max(sc, axis=1, keepdims=True), sk)
            pr = jnp.exp(sc - mx)
            den = jnp.sum(pr, axis=1, keepdims=True) + jnp.exp(sk - mx)
            o = _dot(pr.astype(BF16), vs) / den
            for g, h0 in enumerate(hs):
                o_ref[0, r0:r0 + lc, h0:h0 + A_HD] = o[g * lc:(g + 1) * lc]


def _swa(qa, k_prev, k_own, v_prev, v_own, bias, sink_tab, qb, lc, prev_map, mask_first):
    b, s, _ = qa.shape
    p = WINDOW
    own = lambda w: pl.BlockSpec((1, qb, w), lambda i, j: (i, j, 0))
    prev = pl.BlockSpec((1, p, A_KVW), prev_map)
    full = lambda a: pl.BlockSpec(a.shape, lambda i, j: (0,) * a.ndim)
    return pl.pallas_call(
        functools.partial(_swa_kernel, lc=lc, mask_first=mask_first),
        grid=(b, s // qb),
        in_specs=[own(A_WIDTH), prev, own(A_KVW), prev, own(A_KVW), full(bias), full(sink_tab)],
        out_specs=own(A_WIDTH),
        out_shape=jax.ShapeDtypeStruct((b, s, A_WIDTH), F32),
        scratch_shapes=[pltpu.VMEM((p + qb, A_KVW), BF16), pltpu.VMEM((p + qb, A_KVW), BF16)],
        compiler_params=_params("arbitrary", "arbitrary"),
        name="swa",
    )(qa, k_prev, k_own, v_prev, v_own, bias, sink_tab)


def _outproj_kernel(hm_ref, ha_ref, x_ref, gate_ref, wom_ref, woa_ref, g2_ref, sc_ref, sh_ref, wq_ref,
                    x1_ref, h2_ref, q_ref):
    y = _dot(hm_ref[...].astype(BF16), wom_ref[...]) + _dot(ha_ref[...].astype(BF16), woa_ref[...])
    x1 = x_ref[...] + gate_ref[0] * y
    x1_ref[...] = x1
    h2 = (_rms(x1) * g2_ref[...] * (1.0 + sc_ref[0]) + sh_ref[0]).astype(BF16)
    h2_ref[...] = h2
    q_ref[...] = _dot(h2, wq_ref[...]).astype(BF16)


def _outproj(hm, ha, xf, gate, wom, woa, g2, sc, sh, wq):
    n, d = xf.shape
    t = TOK_TILE
    r = sc.shape[1]
    row = lambda w: pl.BlockSpec((t, w), lambda i: (i, 0))
    full = lambda a: pl.BlockSpec(a.shape, lambda i: (0,) * a.ndim)
    mod = pl.BlockSpec((1, r, d), lambda i: (i, 0, 0))
    nq = wq.shape[1]
    return pl.pallas_call(
        _outproj_kernel,
        grid=(n // t,),
        in_specs=[row(M_WIDTH), row(A_WIDTH), row(d), mod, full(wom), full(woa), full(g2), mod, mod, full(wq)],
        out_specs=[row(d), row(d), row(nq)],
        out_shape=[jax.ShapeDtypeStruct((n, d), F32), jax.ShapeDtypeStruct((n, d), BF16),
                   jax.ShapeDtypeStruct((n, nq), BF16)],
        compiler_params=_params("arbitrary"),
        name="outproj",
    )(hm, ha, xf, gate, wom, woa, g2, sc, sh, wq)


def _top_desc(s, k, want_rank=False):
    vals = []
    cur = s
    rank = jnp.full(s.shape, float(k), F32) if want_rank else None
    for q in range(k):
        m = jnp.max(cur, axis=0, keepdims=True)
        vals.append(m)
        eq = cur == m
        if want_rank:
            rank = jnp.where(eq, float(q), rank)
        cur = jnp.where(eq, NEG, cur)
    return jnp.concatenate(vals, axis=0), rank


def _peer_route(h, q_ref, k1_ref, k2_ref, r2_s, e2_s, c_s, e1_s):
    half = PEER_TOPK // 2
    c0 = h * 2 * PEER_DKH
    s1 = _dot_nt(k1_ref[h], q_ref[:, c0:c0 + PEER_DKH])
    s2 = _dot_nt(k2_ref[h], q_ref[:, c0 + PEER_DKH:c0 + 2 * PEER_DKH])
    a, _ = _top_desc(s1, PEER_TOPK)
    b, rank2 = _top_desc(s2, PEER_TOPK, want_rank=True)
    cand = jnp.concatenate(
        [a[p:p + 1] + b[:half] for p in range(half)]
        + [a[p:p + 1] + b[half:] for p in range(2)]
        + [a[half:] + b[q:q + 1] for q in range(2)], axis=0)
    tau = _top_desc(cand, PEER_TOPK)[0][PEER_TOPK - 1:]
    mx = a[0:1] + b[0:1]
    z = jnp.sum(jnp.where(cand >= tau, jnp.exp(cand - mx), 0.0), axis=0, keepdims=True)
    cnt = jnp.zeros_like(s1)
    for qq in range(PEER_TOPK):
        cnt = cnt + jnp.where(s1 + b[qq:qq + 1] >= tau, 1.0, 0.0)
    r2_s[h] = rank2.astype(BF16)
    e2_s[h] = jnp.exp(s2 - b[0:1]).astype(BF16)
    c_s[h] = cnt
    e1_s[h] = jnp.exp(s1 - a[0:1]) / z


def _peer_kernel(h_ref, q_ref, k1_ref, k2_ref, u_ref, vtp_ref, vtl_ref, x_ref, gate_ref, fg_ref, o_ref,
                 r2_s, e2_s, c_s, e1_s, w_s, acc_s, *, final):
    e = pl.program_id(1)
    tt = h_ref.shape[0]
    eb = u_ref.shape[0]
    groups = eb // N_KEYS
    rows = PEER_ROWS
    slot = lax.rem(e, 2)

    @pl.when(e == 0)
    def _():
        for h in range(PEER_HEADS):
            _peer_route(h, q_ref, k1_ref, k2_ref, r2_s, e2_s, c_s, e1_s)
        acc_s[...] = jnp.zeros_like(acc_s)
        w_s[1] = jnp.zeros(w_s.shape[1:], BF16)

    acc_s[...] += _dot(vtp_ref[...], w_s[1 - slot])
    act = jax.nn.gelu(_dot_nt(u_ref[...], h_ref[...]), approximate=True).astype(BF16)
    zero = jnp.zeros((rows, tt), BF16)
    for ii in range(groups):
        i = e * groups + ii
        g = [zero] * (N_KEYS // rows)
        for h in range(PEER_HEADS):
            cb = jnp.broadcast_to(c_s[h, pl.ds(i, 1), :], (rows, tt)).astype(BF16)
            e1b = jnp.broadcast_to(e1_s[h, pl.ds(i, 1), :], (rows, tt)).astype(BF16)
            for jc in range(N_KEYS // rows):
                sel = r2_s[h, jc * rows:(jc + 1) * rows, :] < cb
                g[jc] = g[jc] + jnp.where(sel, e2_s[h, jc * rows:(jc + 1) * rows, :], zero) * e1b
        for jc in range(N_KEYS // rows):
            r0 = ii * N_KEYS + jc * rows
            w_s[slot, r0:r0 + rows, :] = g[jc] * act[r0:r0 + rows]

    @pl.when(e == pl.num_programs(1) - 1)
    def _():
        acc = acc_s[...] + _dot(vtl_ref[...], w_s[slot])
        x2 = x_ref[...] + gate_ref[0] * acc.T
        if final:
            x2 = _rms(x2) * fg_ref[...]
        o_ref[...] = x2


def _peer(h2, q, k1, k2, u, vt, x1, gate, fg, final):
    n, d = x1.shape
    t = PEER_TOK
    eb = PEER_EB
    nb = u.shape[0] // eb
    r = gate.shape[1]
    row = lambda w: pl.BlockSpec((t, w), lambda i, e: (i, 0))
    full = lambda a: pl.BlockSpec(a.shape, lambda i, e: (0,) * a.ndim)
    tab = lambda dt: pltpu.VMEM((PEER_HEADS, N_KEYS, t), dt)
    return pl.pallas_call(
        functools.partial(_peer_kernel, final=final),
        grid=(n // t, nb),
        in_specs=[row(d), row(q.shape[1]), full(k1), full(k2),
                  pl.BlockSpec((eb, d), lambda i, e: (e, 0)),
                  pl.BlockSpec((d, eb), lambda i, e: (0, jnp.maximum(e - 1, 0))),
                  pl.BlockSpec((d, eb), lambda i, e: (0, nb - 1)),
                  row(d), pl.BlockSpec((1, r, d), lambda i, e: (i, 0, 0)), full(fg)],
        out_specs=row(d),
        out_shape=jax.ShapeDtypeStruct((n, d), F32),
        scratch_shapes=[tab(BF16), tab(BF16), tab(F32), tab(F32),
                        pltpu.VMEM((2, eb, t), BF16),
                        pltpu.VMEM((d, t), F32)],
        compiler_params=_params("arbitrary", "arbitrary"),
        name="peer",
    )(h2, q, k1, k2, u, vt, vt, x1, gate, fg)


def _layer_weights(l, norm1_g, norm2_g, w_in, b_igate, b_fgate, mnorm_g, sinks, w_out,
                   peer_wq, peer_k1, peer_k2, peer_u, peer_v):
    d = w_in.shape[1]
    w = w_in[l]
    m4 = 4 * M_WIDTH
    wgate = w[:, m4:m4 + 2 * M_HEADS]
    ghi = wgate.astype(BF16)
    glo = (wgate - ghi.astype(F32)).astype(BF16)
    wg = jnp.zeros((d, 2 * GATE_PAD), BF16)
    wg = wg.at[:, :2 * M_HEADS].set(ghi).at[:, GATE_PAD:GATE_PAD + 2 * M_HEADS].set(glo)
    gb = jnp.zeros((1, GATE_PAD), F32).at[0, :M_HEADS].set(b_igate[l]).at[0, M_HEADS:2 * M_HEADS].set(b_fgate[l])
    return dict(
        g1=norm1_g[l].reshape(1, d), g2=norm2_g[l].reshape(1, d),
        wm=w[:, :m4].astype(BF16), wg=wg, wa=w[:, m4 + 2 * M_HEADS:].astype(BF16), gb=gb,
        mn=mnorm_g[l].reshape(1, M_WIDTH), sinks=sinks[l],
        wom=w_out[l][:M_WIDTH].astype(BF16), woa=w_out[l][M_WIDTH:].astype(BF16),
        wq=peer_wq[l].astype(BF16), k1=peer_k1[l].astype(BF16), k2=peer_k2[l].astype(BF16),
        u=peer_u[l].astype(BF16), vt=peer_v[l].T.astype(BF16))


def _trunk(x, mods, layers, bias, fg, st_c, st_n, st_m, cache_k, cache_v, prompt):
    b, s, d = x.shape
    n = b * s
    xf = x.reshape(n, d)
    lc = CHUNK if prompt else s
    chunk = min(MLSTM_CHUNK, s)
    qb = min(SWA_QB, s)
    outs = []
    for l, lw in enumerate(layers):
        sh1, sc1, gt1, sh2, sc2, gt2 = jnp.split(mods[l], 6, axis=-1)
        tmod = lambda v: _mod_blocks(v, s, TOK_TILE)
        qm, km, vm, om, gts, qa, ka, va = _inproj(xf, lw["g1"], tmod(sc1), tmod(sh1),
                                                  lw["wm"], lw["wg"], lw["wa"], lw["gb"])
        seq = lambda a: a.reshape(b, s, a.shape[-1])
        gts3 = seq(gts)
        gtt = jnp.swapaxes(gts3[:, :, :2 * M_HEADS], 1, 2)
        if prompt:
            c0 = jnp.zeros((b, M_HEADS, M_HD, M_HD), F32)
            n0 = jnp.zeros((b, M_HEADS, M_HD), F32)
            m0 = jnp.zeros((b, M_HEADS, M_HD), F32)
        else:
            c0, n0 = st_c[l].astype(F32), st_n[l].astype(F32)
            m0 = jnp.broadcast_to(st_m[l].astype(F32)[:, :, None], (b, M_HEADS, M_HD))
        hm, c1, n1, m1 = _mlstm(seq(qm), seq(km), seq(vm), seq(om), gts3, gtt, c0, n0, m0, lw["mn"], chunk)
        ka3, va3 = seq(ka), seq(va)
        sink_tab = jnp.broadcast_to(
            jnp.repeat(lw["sinks"].astype(F32).reshape(A_KV, A_GROUP), lc, axis=1)[:, :, None],
            (A_KV, A_GROUP * lc, 128))
        if prompt:
            per = qb // WINDOW
            prev_map = lambda i, j: (i, jnp.maximum(j * per - 1, 0), 0)
            ha = _swa(seq(qa), ka3, ka3, va3, va3, bias, sink_tab, qb, lc, prev_map, True)
            new_k, new_v = ka3[:, -WINDOW:], va3[:, -WINDOW:]
        else:
            ck = cache_k[l].astype(F32).reshape(b, -1, A_KVW)
            cv = cache_v[l].astype(F32).reshape(b, -1, A_KVW)
            ha = _swa(seq(qa), ck, ka3, cv, va3, bias, sink_tab, qb, lc, lambda i, j: (i, 0, 0), False)
            pl_ = ck.shape[1]
            new_k = jnp.concatenate([ck, ka3], axis=1)[:, -pl_:]
            new_v = jnp.concatenate([cv, va3], axis=1)[:, -pl_:]
        x1, h2, q = _outproj(hm.reshape(n, M_WIDTH), ha.reshape(n, A_WIDTH), xf, tmod(gt1),
                             lw["wom"], lw["woa"], lw["g2"], tmod(sc2), tmod(sh2), lw["wq"])
        final = l == len(layers) - 1
        xf = _peer(h2, q, lw["k1"], lw["k2"], lw["u"], lw["vt"], x1, _mod_blocks(gt2, s, PEER_TOK), fg, final)
        outs.append((c1, n1, m1[:, :, 0], new_k.reshape(b, -1, A_KV, A_HD), new_v.reshape(b, -1, A_KV, A_HD)))
    stack = lambda k: jnp.stack([o[k] for o in outs])
    return (xf.reshape(b, s, d), stack(0), stack(1), stack(2), stack(3), stack(4))


def kernel(x_prompt, x_sample, cache_k, cache_v, state_C, state_n, state_m, c_prompt, c_sample, w_ada, b_ada,
           norm1_g, norm2_g, w_in, b_igate, b_fgate, mnorm_g, sinks, rel_bias, w_out, peer_wq, peer_k1, peer_k2,
           peer_u, peer_v, final_g):
    depth = w_in.shape[0]
    d = x_prompt.shape[-1]
    bp, bs = c_prompt.shape[0], c_sample.shape[0]
    c_all = jnp.concatenate([c_prompt, c_sample], axis=0).astype(F32)
    pad = -c_all.shape[0] % 8
    c_all = jnp.pad(c_all, ((0, pad), (0, 0)))
    mods = _ada(c_all, w_ada, b_ada)
    layers = [_layer_weights(l, norm1_g, norm2_g, w_in, b_igate, b_fgate, mnorm_g, sinks, w_out,
                             peer_wq, peer_k1, peer_k2, peer_u, peer_v) for l in range(depth)]
    fg = final_g.reshape(1, d)
    s_len = x_sample.shape[1]
    bias_p = _swa_bias(rel_bias, CHUNK, WINDOW)
    bias_s = _swa_bias(rel_bias, s_len, cache_k.shape[2])
    yp, pc, pn, pm, pk, pv = _trunk(x_prompt, mods[:, :bp], layers, bias_p, fg, None, None, None, None, None, True)
    ys, sc, sn, sm, sk, sv = _trunk(x_sample, mods[:, bp:bp + bs], layers, bias_s, fg,
                                    state_C, state_n, state_m, cache_k, cache_v, False)
    return (yp, ys, pc, pn, pm, pk, pv, sc, sn, sm, sk, sv)
```
